```python
import math
import jax, jax.numpy as jnp
from jax import lax
import numpy as np

D_MODEL = 1024
BATCH = 2
SEQ = 8192
DEPTH = 4

N_META = 16
BLK = 128
PAD_LEAD = BLK - N_META
D_MIX = 2 * D_MODEL
N_MIXERS = 4
D_GROUP = D_MIX // N_MIXERS
HEAD_DIM = 64
N_HG = D_GROUP // HEAD_DIM

FOX_HEADS = N_HG
MLA_HEADS = N_HG
MLA_NOPE = HEAD_DIM
MLA_ROPE = 32
MLA_V = HEAD_DIM
MLA_Q_RANK = D_MODEL // 4
MLA_KV_RANK = D_MODEL // 8
DSA_HEADS = N_HG
DSA_KV = 2
IDX_HEADS = 8
IDX_DIM = 64
TOPK_MAX = 256
SWA_HEADS = N_HG
SWA_KV = 2
WINDOW = 128
REL_BUCKETS = 32
REL_MAX_DIST = 128
ROPE_BASE = 10000.0
LN_EPS = 1e-5
RMS_EPS = 1e-6
NEG = -1e30
ALPHA = (2 * DEPTH) ** 0.25
BETA = (8 * DEPTH) ** -0.25

IN_COLS = (
    ('a_q', FOX_HEADS * HEAD_DIM), ('a_k', FOX_HEADS * HEAD_DIM), ('a_v', FOX_HEADS * HEAD_DIM),
    ('a_f', FOX_HEADS), ('a_g', D_GROUP),
    ('b_cq', MLA_Q_RANK), ('b_ckv', MLA_KV_RANK), ('b_kr', MLA_ROPE), ('b_g', D_GROUP),
    ('c_q', DSA_HEADS * HEAD_DIM), ('c_k', DSA_KV * HEAD_DIM), ('c_v', DSA_KV * HEAD_DIM),
    ('c_iq', IDX_HEADS * IDX_DIM), ('c_ik', IDX_DIM), ('c_iw', IDX_HEADS), ('c_g', D_GROUP),
    ('d_q', SWA_HEADS * HEAD_DIM), ('d_k', SWA_KV * HEAD_DIM), ('d_v', SWA_KV * HEAD_DIM), ('d_g', D_GROUP),
)
VALUE_COLS = ('a_v', 'c_v', 'd_v')
D_IN = sum(s for _, s in IN_COLS)

kernel_name = 'hymba_fox_mla_dsa_swa_deepnorm'


def split_cols(h):
    out = {}
    off = 0
    for name, size in IN_COLS:
        out[name] = h[..., off:off + size]
        off += size
    return out


def layer_norm(x, g, b):
    xf = x.astype(jnp.float32)
    mu = jnp.mean(xf, axis=-1, keepdims=True)
    var = jnp.mean(jnp.square(xf - mu), axis=-1, keepdims=True)
    y = (xf - mu) * lax.rsqrt(var + LN_EPS) * g.astype(jnp.float32) + b.astype(jnp.float32)
    return y.astype(x.dtype)


def rms_norm(x, g):
    xf = x.astype(jnp.float32)
    y = xf * lax.rsqrt(jnp.mean(jnp.square(xf), axis=-1, keepdims=True) + RMS_EPS) * g.astype(jnp.float32)
    return y.astype(x.dtype)


def rope(x, pos):
    half = x.shape[-1] // 2
    freqs = ROPE_BASE ** (-jnp.arange(half, dtype=jnp.float32) / half)
    ang = pos.astype(jnp.float32)[:, None] * freqs[None, :]
    cos = jnp.cos(ang)[:, None, :]
    sin = jnp.sin(ang)[:, None, :]
    x1 = x[..., :half].astype(jnp.float32)
    x2 = x[..., half:].astype(jnp.float32)
    return jnp.concatenate([x1 * cos - x2 * sin, x1 * sin + x2 * cos], axis=-1).astype(x.dtype)


def t5_bucket(dist):
    n = jnp.maximum(dist, 0)
    max_exact = REL_BUCKETS // 2
    nf = jnp.maximum(n, 1).astype(jnp.float32)
    large = max_exact + (jnp.log(nf / max_exact) / math.log(REL_MAX_DIST / max_exact)
                         * (REL_BUCKETS - max_exact)).astype(jnp.int32)
    large = jnp.minimum(large, REL_BUCKETS - 1)
    return jnp.where(n < max_exact, n, large)


def causal_dense_attention(q, k, v, key_valid, scale, decay=None):
    B, P, H, _ = q.shape
    dv = v.shape[-1]
    nb = P // BLK
    kpos = jnp.arange(P)

    def one_block(i):
        start = i * BLK
        qb = lax.dynamic_slice_in_dim(q, start, BLK, axis=1)
        tpos = start + jnp.arange(BLK)
        logits = jnp.einsum('bqhd,bkhd->bhqk', qb, k, preferred_element_type=jnp.float32) * scale
        if decay is not None:
            cq = lax.dynamic_slice_in_dim(decay, start, BLK, axis=2)
            logits = logits + cq[..., None] - decay[:, :, None, :]
        mask = (kpos[None, :] <= tpos[:, None]) & key_valid[None, :]
        probs = jax.nn.softmax(jnp.where(mask, logits, NEG), axis=-1)
        return jnp.einsum('bhqk,bkhd->bqhd', probs.astype(v.dtype), v)

    out = lax.map(one_block, jnp.arange(nb))
    return jnp.transpose(out, (1, 0, 2, 3, 4)).reshape(B, P, H * dv)


def mla_attention(c_q, c_kv, k_rope, g_q, g_kv, w_uq, w_ukv, pos, key_valid):
    B, P, _ = c_q.shape
    q = (rms_norm(c_q, g_q) @ w_uq).reshape(B, P, MLA_HEADS, MLA_NOPE + MLA_ROPE)
    q = jnp.concatenate([q[..., :MLA_NOPE], rope(q[..., MLA_NOPE:], pos)], axis=-1)
    kv = (rms_norm(c_kv, g_kv) @ w_ukv).reshape(B, P, MLA_HEADS, MLA_NOPE + MLA_V)
    k_pe = jnp.broadcast_to(rope(k_rope[:, :, None, :], pos), (B, P, MLA_HEADS, MLA_ROPE))
    k = jnp.concatenate([kv[..., :MLA_NOPE], k_pe], axis=-1)
    v = kv[..., MLA_NOPE:]
    return causal_dense_attention(q, k, v, key_valid, (MLA_NOPE + MLA_ROPE) ** -0.5)


def dsa_sparse_attention(q, k, v, iq, ik, iw, key_valid, rel_bias_c, top_k):
    B, P, G, R, d = q.shape
    nb = P // BLK
    kpos = jnp.arange(P)
    gather = jax.vmap(lambda arr, idx: arr[idx])

    def one_block(i):
        start = i * BLK
        tpos = start + jnp.arange(BLK)
        qb = lax.dynamic_slice_in_dim(q, start, BLK, axis=1)
        iqb = lax.dynamic_slice_in_dim(iq, start, BLK, axis=1)
        iwb = lax.dynamic_slice_in_dim(iw, start, BLK, axis=1)
        s = jnp.einsum('bqhd,bkd->bqhk', iqb, ik, preferred_element_type=jnp.float32) * IDX_DIM ** -0.5
        score = jnp.einsum('bqhk,bqh->bqk', jax.nn.relu(s), iwb.astype(jnp.float32)) * IDX_HEADS ** -0.5
        adm = (kpos[None, :] <= tpos[:, None]) & key_valid[None, :]
        score = jnp.where(adm[None], score, NEG)
        _, idx = lax.top_k(score, top_k)
        k_sel = gather(k, idx)
        v_sel = gather(v, idx)
        logits = jnp.einsum('bqgrd,bqkgd->bgrqk', qb, k_sel, preferred_element_type=jnp.float32) * d ** -0.5
        bias = rel_bias_c[t5_bucket(tpos[None, :, None] - idx)].astype(jnp.float32)
        bias = bias.reshape(B, BLK, top_k, G, R).transpose(0, 3, 4, 1, 2)
        sel_ok = (idx <= tpos[None, :, None]) & key_valid[idx]
        logits = jnp.where(sel_ok[:, None, None], logits + bias, NEG)
        probs = jax.nn.softmax(logits, axis=-1)
        return jnp.einsum('bgrqk,bqkgd->bqgrd', probs.astype(v.dtype), v_sel)

    out = lax.map(one_block, jnp.arange(nb))
    return jnp.transpose(out, (1, 0, 2, 3, 4, 5)).reshape(B, P, G * R * d)


def sliding_window_sink_attention(q, k, v, key_valid, rel_bias_d, sinks):
    B, P, G, R, d = q.shape
    nb = P // BLK
    qb = q.reshape(B, nb, BLK, G, R, d)
    kb = k.reshape(B, nb, BLK, G, d)
    vb = v.reshape(B, nb, BLK, G, d)
    zero = jnp.zeros_like(kb[:, :1])
    kk = jnp.concatenate([jnp.concatenate([zero, kb[:, :-1]], axis=1), kb], axis=2)
    vv = jnp.concatenate([jnp.concatenate([zero, vb[:, :-1]], axis=1), vb], axis=2)
    qi = jnp.arange(BLK)
    kj = jnp.arange(2 * BLK)
    dist = BLK + qi[:, None] - kj[None, :]
    kabs = (jnp.arange(nb)[:, None] - 1) * BLK + kj[None, :]
    kval = (kabs >= 0) & key_valid[jnp.clip(kabs, 0, P - 1)]
    mask = ((dist >= 0) & (dist < WINDOW))[None] & kval[:, None, :]
    bias = rel_bias_d[t5_bucket(dist)].astype(jnp.float32)
    bias = bias.reshape(BLK, 2 * BLK, G, R).transpose(2, 3, 0, 1)
    logits = jnp.einsum('bnqgrd,bnkgd->bngrqk', qb, kk, preferred_element_type=jnp.float32) * d ** -0.5 + bias
    logits = jnp.where(mask[None, :, None, None], logits, NEG)
    sink = sinks.astype(jnp.float32).reshape(G, R)[None, None, :, :, None, None]
    m = jnp.maximum(jnp.max(logits, axis=-1, keepdims=True), sink)
    e = jnp.exp(logits - m)
    probs = e / (jnp.sum(e, axis=-1, keepdims=True) + jnp.exp(sink - m))
    out = jnp.einsum('bngrqk,bnkgd->bnqgrd', probs.astype(v.dtype), vv)
    return out.reshape(B, P, G * R * d)


def hybrid_layer(x, pos, key_valid, rel_bias, top_k, w_in, b_f, g_q, g_kv, w_uq, w_ukv, sinks, w_out, ln_g, ln_b):
    B, P, _ = x.shape
    h = split_cols(x @ w_in)
    qa = h['a_q'].reshape(B, P, FOX_HEADS, HEAD_DIM)
    ka = h['a_k'].reshape(B, P, FOX_HEADS, HEAD_DIM)
    va = h['a_v'].reshape(B, P, FOX_HEADS, HEAD_DIM)
    logf = jax.nn.log_sigmoid(h['a_f'].astype(jnp.float32) + b_f.astype(jnp.float32))
    decay = jnp.transpose(jnp.cumsum(logf, axis=1), (0, 2, 1))
    out_a = causal_dense_attention(qa, ka, va, key_valid, HEAD_DIM ** -0.5, decay)
    out_b = mla_attention(h['b_cq'], h['b_ckv'], h['b_kr'], g_q, g_kv, w_uq, w_ukv, pos, key_valid)
    rc = DSA_HEADS // DSA_KV
    out_c = dsa_sparse_attention(
        h['c_q'].reshape(B, P, DSA_KV, rc, HEAD_DIM),
        h['c_k'].reshape(B, P, DSA_KV, HEAD_DIM),
        h['c_v'].reshape(B, P, DSA_KV, HEAD_DIM),
        h['c_iq'].reshape(B, P, IDX_HEADS, IDX_DIM), h['c_ik'], h['c_iw'],
        key_valid, rel_bias[:, :DSA_HEADS], top_k)
    rd = SWA_HEADS // SWA_KV
    out_d = sliding_window_sink_attention(
        h['d_q'].reshape(B, P, SWA_KV, rd, HEAD_DIM),
        h['d_k'].reshape(B, P, SWA_KV, HEAD_DIM),
        h['d_v'].reshape(B, P, SWA_KV, HEAD_DIM),
        key_valid, rel_bias[:, DSA_HEADS:], sinks)
    mixed = jnp.concatenate([
        out_a * jax.nn.silu(h['a_g']), out_b * jax.nn.silu(h['b_g']),
        out_c * jax.nn.silu(h['c_g']), out_d * jax.nn.silu(h['d_g'])], axis=-1)
    y = mixed @ w_out
    return layer_norm(ALPHA * x + y, ln_g, ln_b)


def setup_inputs(seed: int = 0) -> dict:
    key = jax.random.key(seed)
    ks = jax.random.split(key, 16)
    f32 = jnp.float32
    col_scale = jnp.asarray(np.concatenate(
        [np.full((s,), BETA if name in VALUE_COLS else 1.0, dtype=np.float32) for name, s in IN_COLS]))
    ukv_scale = jnp.concatenate([jnp.ones((MLA_HEADS, MLA_NOPE), f32),
                                 jnp.full((MLA_HEADS, MLA_V), BETA, f32)], axis=-1).reshape(-1)
    return {
        'x': jax.random.normal(ks[0], (BATCH, SEQ, D_MODEL), f32),
        'meta_tokens': jax.random.normal(ks[1], (N_META, D_MODEL), f32),
        'ln0_g': 1.0 + 0.02 * jax.random.normal(ks[2], (D_MODEL,), f32),
        'ln0_b': 0.02 * jax.random.normal(ks[3], (D_MODEL,), f32),
        'rel_bias': 0.5 * jax.random.normal(ks[4], (REL_BUCKETS, DSA_HEADS + SWA_HEADS), f32),
        'w_in': jax.random.normal(ks[5], (DEPTH, D_MODEL, D_IN), f32) * D_MODEL ** -0.5 * col_scale,
        'b_f': 1.0 + 0.1 * jax.random.normal(ks[6], (DEPTH, FOX_HEADS), f32),
        'mla_gq': 1.0 + 0.02 * jax.random.normal(ks[7], (DEPTH, MLA_Q_RANK), f32),
        'mla_gkv': 1.0 + 0.02 * jax.random.normal(ks[8], (DEPTH, MLA_KV_RANK), f32),
        'w_uq': jax.random.normal(ks[9], (DEPTH, MLA_Q_RANK, MLA_HEADS * (MLA_NOPE + MLA_ROPE)), f32) * MLA_Q_RANK ** -0.5,
        'w_ukv': jax.random.normal(ks[10], (DEPTH, MLA_KV_RANK, MLA_HEADS * (MLA_NOPE + MLA_V)), f32) * MLA_KV_RANK ** -0.5 * ukv_scale,
        'sinks': 0.5 * jax.random.normal(ks[11], (DEPTH, SWA_HEADS), f32),
        'w_out': jax.random.normal(ks[12], (DEPTH, D_MIX, D_MODEL), f32) * D_MIX ** -0.5 * BETA,
        'ln_g': 1.0 + 0.02 * jax.random.normal(ks[13], (DEPTH, D_MODEL), f32),
        'ln_b': 0.02 * jax.random.normal(ks[14], (DEPTH, D_MODEL), f32),
    }


def reference(x, meta_tokens, ln0_g, ln0_b, rel_bias, w_in, b_f, mla_gq, mla_gkv, w_uq, w_ukv, sinks, w_out, ln_g, ln_b):
    B, S, D = x.shape
    lead = jnp.concatenate([jnp.zeros((B, PAD_LEAD, D), x.dtype),
                            jnp.broadcast_to(meta_tokens[None].astype(x.dtype), (B, N_META, D))], axis=1)
    h = jnp.concatenate([lead, x], axis=1)
    P = h.shape[1]
    pos = jnp.arange(P) - PAD_LEAD
    key_valid = pos >= 0
    top_k = min(TOPK_MAX, S // 4)
    h = layer_norm(h, ln0_g, ln0_b)
    for l in range(DEPTH):
        h = hybrid_layer(h, pos, key_valid, rel_bias, top_k, w_in[l], b_f[l], mla_gq[l], mla_gkv[l],
                         w_uq[l], w_ukv[l], sinks[l], w_out[l], ln_g[l], ln_b[l])
    return h[:, BLK:]
```

```python
import functools
import math

import numpy as np
import jax
import jax.numpy as jnp
from jax import lax
from jax.experimental import pallas as pl
from jax.experimental.pallas import tpu as pltpu

F32 = jnp.float32
BF16 = jnp.bfloat16

LANES = 128
VMEM_LIMIT = 56 * 1024 * 1024

N_META = 16
BLK = 128
PAD_LEAD = BLK - N_META
HEAD_DIM = 64
N_HEADS = 8
D_GROUP = N_HEADS * HEAD_DIM
MLA_ROPE = 32
MLA_Q_RANK = 256
MLA_KV_RANK = 128
KV_GROUPS = 2
GROUP_HEADS = N_HEADS // KV_GROUPS
IDX_HEADS = 8
IDX_DIM = 64
TOPK_MAX = 256
WINDOW = 128
REL_BUCKETS = 32
REL_MAX_DIST = 128
ROPE_BASE = 10000.0
LN_EPS = 1e-5
RMS_EPS = 1e-6
NEG = -1e30
DEPTH_FOR_NORM = 4
ALPHA = (2 * DEPTH_FOR_NORM) ** 0.25
INT_MIN = -2 ** 31

_IN_COLS = (
    ('a_q', 512), ('a_k', 512), ('a_v', 512), ('a_f', 8), ('a_g', 512),
    ('b_cq', 256), ('b_ckv', 128), ('b_kr', 32), ('b_g', 512),
    ('c_q', 512), ('c_k', 128), ('c_v', 128), ('c_iq', 512), ('c_ik', 64), ('c_iw', 8), ('c_g', 512),
    ('d_q', 512), ('d_k', 128), ('d_v', 128), ('d_g', 512),
)
_OFF = {}
_o = 0
for _n, _s in _IN_COLS:
    _OFF[_n] = (_o, _s)
    _o += _s

W_A = 1536
W_G = 2048
W_MLA = 512
W_C = 1408
W_IW = 128
W_D = 768
W_TOTAL = W_A + W_G + W_MLA + W_C + W_IW + W_D
F_ROWS = 16


def _tile(n, cap):
    best = None
    for t in range(LANES, cap + 1, LANES):
        if n % t == 0:
            best = t
    assert best is not None, (n, cap)
    return best


def _params(sem):
    return pltpu.CompilerParams(dimension_semantics=sem, vmem_limit_bytes=VMEM_LIMIT)


def _t5_bucket_np(dist):
    n = np.maximum(dist, 0)
    max_exact = REL_BUCKETS // 2
    nf = np.maximum(n, 1).astype(np.float32)
    large = max_exact + (np.log(nf / np.float32(max_exact)) / np.float32(math.log(REL_MAX_DIST / max_exact))
                         * np.float32(REL_BUCKETS - max_exact)).astype(np.int32)
    large = np.minimum(large, REL_BUCKETS - 1)
    return np.where(n < max_exact, n, large).astype(np.int32)


def _ln_kernel(x_ref, g_ref, b_ref, o_ref):
    x = x_ref[...]
    mu = jnp.mean(x, axis=-1, keepdims=True)
    xc = x - mu
    var = jnp.mean(xc * xc, axis=-1, keepdims=True)
    o_ref[...] = xc * lax.rsqrt(var + LN_EPS) * g_ref[...] + b_ref[...]


def _layer_norm(x2, g, b):
    n, d = x2.shape
    tm = _tile(n, 640)
    return pl.pallas_call(
        _ln_kernel,
        out_shape=jax.ShapeDtypeStruct((n, d), F32),
        grid=(n // tm,),
        in_specs=[pl.BlockSpec((tm, d), lambda i: (i, 0)),
                  pl.BlockSpec((1, d), lambda i: (0, 0)),
                  pl.BlockSpec((1, d), lambda i: (0, 0))],
        out_specs=pl.BlockSpec((tm, d), lambda i: (i, 0)),
        compiler_params=_params(("parallel",)),
        name="ln0",
    )(x2, g.reshape(1, d), b.reshape(1, d))


def _bias_table_kernel(relb_ref, bucket_ref, o_ref):
    bk = bucket_ref[...]
    for h in range(o_ref.shape[0]):
        acc = jnp.zeros(bk.shape, F32)
        for b in range(REL_BUCKETS):
            acc = jnp.where(bk == b, relb_ref[b, h], acc)
        o_ref[h] = acc


def _bias_tables(rel_bias):
    nh = rel_bias.shape[1]
    q = np.arange(BLK)[:, None]
    j = np.arange(2 * BLK)[None, :]
    bucket = jnp.asarray(_t5_bucket_np(BLK + q - j))
    return pl.pallas_call(
        _bias_table_kernel,
        out_shape=jax.ShapeDtypeStruct((nh, BLK, 2 * BLK), F32),
        in_specs=[pl.BlockSpec(memory_space=pltpu.SMEM),
                  pl.BlockSpec((BLK, 2 * BLK), lambda: (0, 0))],
        out_specs=pl.BlockSpec((nh, BLK, 2 * BLK), lambda: (0, 0, 0)),
        name="bias_tables",
    )(rel_bias, bucket)


def _inproj_kernel(x_ref, w_ref, wf_ref, a_ref, g_ref, mla_ref, c_ref, iw_ref, d_ref, ft_ref):
    xb = x_ref[...].astype(BF16)
    col = 0
    for ref, width in ((a_ref, W_A), (g_ref, W_G), (mla_ref, W_MLA), (c_ref, W_C), (None, W_IW), (d_ref, W_D)):
        for c0 in range(0, width, 512):
            c1 = min(c0 + 512, width)
            y = jnp.dot(xb, w_ref[:, col + c0:col + c1], preferred_element_type=F32)
            if ref is None:
                iw_ref[...] = y[:, :IDX_HEADS]
            else:
                ref[:, c0:c1] = y.astype(ref.dtype)
        col += width
    ft_ref[...] = lax.dot_general(wf_ref[...], xb, (((1,), (1,)), ((), ())), preferred_element_type=F32)


def _inproj(h2, w_p, wf_t):
    n, d = h2.shape
    tm = _tile(n, 256)
    row = lambda i: (i, 0)
    outs = [
        jax.ShapeDtypeStruct((n, W_A), BF16),
        jax.ShapeDtypeStruct((n, W_G), F32),
        jax.ShapeDtypeStruct((n, W_MLA), F32),
        jax.ShapeDtypeStruct((n, W_C), BF16),
        jax.ShapeDtypeStruct((n, IDX_HEADS), F32),
        jax.ShapeDtypeStruct((n, W_D), BF16),
        jax.ShapeDtypeStruct((F_ROWS, n), F32),
    ]
    return pl.pallas_call(
        _inproj_kernel,
        out_shape=outs,
        grid=(n // tm,),
        in_specs=[pl.BlockSpec((tm, d), row),
                  pl.BlockSpec((d, W_TOTAL), lambda i: (0, 0)),
                  pl.BlockSpec((F_ROWS, d), lambda i: (0, 0))],
        out_specs=[pl.BlockSpec((tm, W_A), row), pl.BlockSpec((tm, W_G), row), pl.BlockSpec((tm, W_MLA), row),
                   pl.BlockSpec((tm, W_C), row), pl.BlockSpec((tm, IDX_HEADS), row), pl.BlockSpec((tm, W_D), row),
                   pl.BlockSpec((F_ROWS, tm), lambda i: (0, i))],
        compiler_params=_params(("parallel",)),
        name="inproj",
    )(h2, w_p, wf_t)


def _prep_w_in(w_in_l):
    def cols(name):
        o, s = _OFF[name]
        return w_in_l[:, o:o + s]
    d = w_in_l.shape[0]
    z = lambda k: jnp.zeros((d, k), w_in_l.dtype)
    kr = cols('b_kr')
    half = MLA_ROPE // 2
    kr_rot = jnp.concatenate([-kr[:, half:], kr[:, :half]], axis=1)
    qs = HEAD_DIM ** -0.5
    parts = [
        cols('a_q') * qs, cols('a_k'), cols('a_v'),
        cols('a_g'), cols('b_g'), cols('c_g'), cols('d_g'),
        cols('b_cq'), cols('b_ckv'), z(64), kr, kr_rot,
        cols('c_q') * qs, cols('c_iq') * (IDX_DIM ** -0.5), cols('c_k'), cols('c_v'), cols('c_ik'), z(64),
        cols('c_iw'), z(W_IW - IDX_HEADS),
        cols('d_q') * qs, cols('d_k'), cols('d_v'),
    ]
    w_p = jnp.concatenate(parts, axis=1).astype(BF16)
    assert w_p.shape[1] == W_TOTAL
    wf_t = jnp.concatenate([cols('a_f').T, jnp.zeros((F_ROWS - N_HEADS, d), w_in_l.dtype)], axis=0).astype(BF16)
    return w_p, wf_t


def _decay_kernel(ft_ref, bf_ref, ck_ref):
    p = ft_ref.shape[1]
    lane = lax.broadcasted_iota(jnp.int32, (F_ROWS, LANES), 1)

    def body(c, carry):
        off = pl.multiple_of(c * LANES, LANES)
        z = ft_ref[:, pl.ds(off, LANES)] + bf_ref[...]
        s = -(jnp.maximum(-z, 0.0) + jnp.log1p(jnp.exp(-jnp.abs(z))))
        for sh in (1, 2, 4, 8, 16, 32, 64):
            s = s + jnp.where(lane >= sh, pltpu.roll(s, sh, axis=1), 0.0)
        s = s + carry
        ck_ref[0, :, pl.ds(off, LANES)] = s
        return s[:, LANES - 1:LANES]

    lax.fori_loop(0, p // LANES, body, jnp.zeros((F_ROWS, 1), F32))


def _decay(ft, b_f_l, batch, p):
    bf = jnp.concatenate([b_f_l, jnp.zeros((F_ROWS - N_HEADS,), F32)]).reshape(F_ROWS, 1)
    return pl.pallas_call(
        _decay_kernel,
        out_shape=jax.ShapeDtypeStruct((batch, F_ROWS, p), F32),
        grid=(batch,),
        in_specs=[pl.BlockSpec((F_ROWS, p), lambda b: (0, b)),
                  pl.BlockSpec((F_ROWS, 1), lambda b: (0, 0))],
        out_specs=pl.BlockSpec((1, F_ROWS, p), lambda b: (b, 0, 0)),
        compiler_params=_params(("parallel",)),
        name="fox_decay",
    )(ft, bf)


def _mla_prep_kernel(x_ref, gq_ref, gkv_ref, wq_ref, wk_ref, wv_ref, c1_ref, c2_ref, q_ref, k_ref, v_ref, *, scale):
    cq = x_ref[:, :MLA_Q_RANK]
    ckv = x_ref[:, MLA_Q_RANK:MLA_Q_RANK + MLA_KV_RANK]
    kr = x_ref[:, MLA_Q_RANK + MLA_KV_RANK:]
    cqn = (cq * lax.rsqrt(jnp.mean(cq * cq, axis=-1, keepdims=True) + RMS_EPS) * gq_ref[...]).astype(BF16)
    ckvn = (ckv * lax.rsqrt(jnp.mean(ckv * ckv, axis=-1, keepdims=True) + RMS_EPS) * gkv_ref[...]).astype(BF16)
    c1 = c1_ref[...]
    c2 = c2_ref[...]
    k_pe = kr * c1 + pltpu.roll(kr, LANES - MLA_ROPE, axis=1) * c2
    for h in range(N_HEADS):
        sl = slice(h * LANES, (h + 1) * LANES)
        yq = jnp.dot(cqn, wq_ref[:, sl], preferred_element_type=F32)
        yq = yq * c1 + pltpu.roll(yq, LANES - MLA_ROPE, axis=1) * c2
        q_ref[:, sl] = (yq * scale).astype(BF16)
        yk = jnp.dot(ckvn, wk_ref[:, sl], preferred_element_type=F32)
        k_ref[:, sl] = (yk + k_pe).astype(BF16)
    v_ref[...] = jnp.dot(ckvn, wv_ref[...], preferred_element_type=F32).astype(BF16)


def _mla_prep(mla_in, gq, gkv, wq_p, wk_p, wv_p, c1, c2, p):
    n = mla_in.shape[0]
    tm = _tile(p, 640)
    npb = p // tm
    row = lambda i: (i, 0)
    const = lambda i: (0, 0)
    hw = N_HEADS * LANES
    return pl.pallas_call(
        functools.partial(_mla_prep_kernel, scale=(HEAD_DIM + MLA_ROPE) ** -0.5),
        out_shape=[jax.ShapeDtypeStruct((n, hw), BF16), jax.ShapeDtypeStruct((n, hw), BF16),
                   jax.ShapeDtypeStruct((n, D_GROUP), BF16)],
        grid=(n // tm,),
        in_specs=[pl.BlockSpec((tm, W_MLA), row),
                  pl.BlockSpec((1, MLA_Q_RANK), const), pl.BlockSpec((1, MLA_KV_RANK), const),
                  pl.BlockSpec((MLA_Q_RANK, hw), const), pl.BlockSpec((MLA_KV_RANK, hw), const),
                  pl.BlockSpec((MLA_KV_RANK, D_GROUP), const),
                  pl.BlockSpec((tm, LANES), lambda i: (i % npb, 0)),
                  pl.BlockSpec((tm, LANES), lambda i: (i % npb, 0))],
        out_specs=[pl.BlockSpec((tm, hw), row), pl.BlockSpec((tm, hw), row), pl.BlockSpec((tm, D_GROUP), row)],
        compiler_params=_params(("parallel",)),
        name="mla_prep",
    )(mla_in, gq.reshape(1, -1), gkv.reshape(1, -1), wq_p, wk_p, wv_p, c1, c2)


def _prep_mla_weights(w_uq_l, w_ukv_l):
    half = MLA_ROPE // 2
    wq = w_uq_l.reshape(MLA_Q_RANK, N_HEADS, HEAD_DIM + MLA_ROPE)
    pe = wq[..., HEAD_DIM:]
    pe_rot = jnp.concatenate([-pe[..., half:], pe[..., :half]], axis=-1)
    wq_p = jnp.concatenate([wq, pe_rot], axis=-1).reshape(MLA_Q_RANK, N_HEADS * LANES).astype(BF16)
    wkv = w_ukv_l.reshape(MLA_KV_RANK, N_HEADS, 2 * HEAD_DIM)
    wk_p = jnp.concatenate([wkv[..., :HEAD_DIM], jnp.zeros_like(wkv[..., :HEAD_DIM])], axis=-1)
    wk_p = wk_p.reshape(MLA_KV_RANK, N_HEADS * LANES).astype(BF16)
    wv_p = wkv[..., HEAD_DIM:].reshape(MLA_KV_RANK, D_GROUP).astype(BF16)
    return wq_p, wk_p, wv_p


def _rope_tables(p):
    half = MLA_ROPE // 2
    pos = (jnp.arange(p) - PAD_LEAD).astype(F32)
    freqs = ROPE_BASE ** (-jnp.arange(half, dtype=F32) / half)
    ang = pos[:, None] * freqs[None, :]
    cos = jnp.cos(ang)
    sin = jnp.sin(ang)
    one = jnp.ones((p, HEAD_DIM), F32)
    z64 = jnp.zeros((p, HEAD_DIM), F32)
    z32 = jnp.zeros((p, MLA_ROPE), F32)
    c1 = jnp.concatenate([one, cos, cos, z32], axis=1)
    c2 = jnp.concatenate([z64, sin, sin, z32], axis=1)
    return c1, c2


def _flash_kernel(*refs, heads, dk, dv, t, use_decay):
    if use_decay:
        q_ref, k_ref, v_ref, ck_ref, o_ref, m_sc, l_sc, acc_sc = refs
    else:
        q_ref, k_ref, v_ref, o_ref, m_sc, l_sc, acc_sc = refs
        ck_ref = None
    i = pl.program_id(1)
    j = pl.program_id(2)

    @pl.when(j == 0)
    def _():
        m_sc[...] = jnp.full(m_sc.shape, -jnp.inf, F32)
        l_sc[...] = jnp.zeros(l_sc.shape, F32)
        acc_sc[...] = jnp.zeros(acc_sc.shape, F32)

    def step(masked):
        if masked:
            tpos = i * t + lax.broadcasted_iota(jnp.int32, (t, 1), 0)
            kpos = j * t + lax.broadcasted_iota(jnp.int32, (1, t), 1)
            mask = (kpos <= tpos) & (kpos >= PAD_LEAD)
        for h in range(heads):
            q = q_ref[0, :, h * dk:(h + 1) * dk]
            k = k_ref[0, :, h * dk:(h + 1) * dk]
            s = lax.dot_general(q, k, (((1,), (1,)), ((), ())), preferred_element_type=F32)
            if use_decay:
                s = s - ck_ref[0, h:h + 1, :]
            if masked:
                s = jnp.where(mask, s, NEG)
            m_prev = m_sc[h]
            m_new = jnp.maximum(m_prev, jnp.max(s, axis=-1, keepdims=True))
            alpha = jnp.exp(m_prev - m_new)
            p = jnp.exp(s - m_new)
            l_sc[h] = alpha * l_sc[h] + jnp.sum(p, axis=-1, keepdims=True)
            pv = jnp.dot(p.astype(BF16), v_ref[0, :, h * dv:(h + 1) * dv], preferred_element_type=F32)
            acc_sc[:, h * dv:(h + 1) * dv] = alpha * acc_sc[:, h * dv:(h + 1) * dv] + pv
            m_sc[h] = m_new

    @pl.when((j == i) | (j == 0))
    def _():
        step(True)

    @pl.when((j < i) & (j > 0))
    def _():
        step(False)

    @pl.when(j == i)
    def _():
        for h in range(heads):
            o_ref[0, :, h * dv:(h + 1) * dv] = acc_sc[:, h * dv:(h + 1) * dv] / l_sc[h]


def _flash(q, k, v, ck, *, dk, name):
    (q, qc), (k, kc), (v, vc) = q, k, v
    b, p, _ = q.shape
    dv = HEAD_DIM
    t = _tile(p, 640)
    nt = p // t
    qmap = lambda bb, i, j: (bb, i, 0)
    in_specs = [pl.BlockSpec((1, t, N_HEADS * dk), lambda bb, i, j: (bb, i, qc)),
                pl.BlockSpec((1, t, N_HEADS * dk), lambda bb, i, j: (bb, jnp.minimum(i, j), kc)),
                pl.BlockSpec((1, t, N_HEADS * dv), lambda bb, i, j: (bb, jnp.minimum(i, j), vc))]
    args = [q, k, v]
    if ck is not None:
        in_specs.append(pl.BlockSpec((1, F_ROWS, t), lambda bb, i, j: (bb, 0, jnp.minimum(i, j))))
        args.append(ck)
    return pl.pallas_call(
        functools.partial(_flash_kernel, heads=N_HEADS, dk=dk, dv=dv, t=t, use_decay=ck is not None),
        out_shape=jax.ShapeDtypeStruct((b, p, N_HEADS * dv), F32),
        grid=(b, nt, nt),
        in_specs=in_specs,
        out_specs=pl.BlockSpec((1, t, N_HEADS * dv), qmap),
        scratch_shapes=[pltpu.VMEM((N_HEADS, t, 1), F32), pltpu.VMEM((N_HEADS, t, 1), F32),
                        pltpu.VMEM((t, N_HEADS * dv), F32)],
        compiler_params=_params(("parallel", "parallel", "arbitrary")),
        name=name,
    )(*args)


def _dsa_kernel(relb_ref, q_ref, iq_ref, iw_ref, k_ref, v_ref, ik_ref, tbl_ref, o_ref,
                key_sc, j_sc, m_sc, l_sc, acc_sc, *, ch, top_k, far_bucket):
    i = pl.program_id(1)
    tpos = i * BLK + lax.broadcasted_iota(jnp.int32, (BLK, 1), 0)
    nc = ((i + 1) * BLK + ch - 1) // ch
    ntile = ch // LANES

    def score_chunk(c, carry):
        off = pl.multiple_of(c * ch, LANES)
        ikc = ik_ref[0, pl.ds(off, ch), :IDX_DIM]
        acc = jnp.zeros((BLK, ch), F32)
        for h in range(IDX_HEADS):
            s = lax.dot_general(iq_ref[0, :, h * IDX_DIM:(h + 1) * IDX_DIM], ikc,
                                (((1,), (1,)), ((), ())), preferred_element_type=F32)
            acc = acc + jnp.maximum(s, 0.0) * iw_ref[0, :, h:h + 1]
        acc = acc * (IDX_HEADS ** -0.5)
        kpos = off + lax.broadcasted_iota(jnp.int32, (1, ch), 1)
        sc = jnp.where((kpos <= tpos) & (kpos >= PAD_LEAD), acc, NEG)
        sc = jnp.where(sc == 0.0, 0.0, sc)
        bits = pltpu.bitcast(sc, jnp.int32)
        key_sc[:, pl.ds(off, ch)] = bits ^ ((bits >> 31) & 0x7FFFFFFF)
        return carry

    lax.fori_loop(0, nc, score_chunk, 0)

    def count(pred):
        def body(c, cnt):
            off = pl.multiple_of(c * ch, LANES)
            for tt in range(ntile):
                kk = key_sc[:, pl.ds(off + tt * LANES, LANES)]
                kpos = off + tt * LANES + lax.broadcasted_iota(jnp.int32, (1, LANES), 1)
                cnt = cnt + jnp.where(pred(kk, kpos), 1.0, 0.0)
            return cnt
        cnt = lax.fori_loop(0, nc, body, jnp.zeros((BLK, LANES), F32))
        return jnp.sum(cnt, axis=-1, keepdims=True)

    kf = float(top_k)

    def bit_body(b, thr):
        cand = thr + jnp.left_shift(jnp.int32(1), 31 - b)
        cnt = count(lambda kk, kpos: kk >= cand)
        return jnp.where(cnt >= kf, cand, thr)

    thr = lax.fori_loop(0, 32, bit_body, jnp.full((BLK, 1), INT_MIN, jnp.int32))

    cnt_ge = count(lambda kk, kpos: kk >= thr)
    cnt_gt = count(lambda kk, kpos: kk > thr)
    need = cnt_ge > kf
    big = jnp.int32(2 ** 30)
    j_sc[...] = jnp.full((BLK, 1), big, jnp.int32)

    @pl.when(jnp.max(jnp.where(need, 1.0, 0.0)) > 0.0)
    def _():
        room = kf - cnt_gt - 1.0
        nbits = max(1, int(key_sc.shape[1]).bit_length())

        def jbody(b, jj):
            cand = jj + jnp.left_shift(jnp.int32(1), nbits - 1 - b)
            cnt = count(lambda kk, kpos: (kk == thr) & (kpos < cand))
            return jnp.where(cnt <= room, cand, jj)

        jj = lax.fori_loop(0, nbits, jbody, jnp.zeros((BLK, 1), jnp.int32))
        j_sc[...] = jnp.where(need, jj, big)

    jrow = j_sc[...]

    m_sc[...] = jnp.full(m_sc.shape, -jnp.inf, F32)
    l_sc[...] = jnp.zeros(l_sc.shape, F32)
    acc_sc[...] = jnp.zeros(acc_sc.shape, F32)

    def attend(off, width, limit_fn, bias_fn):
        kk = key_sc[:, pl.ds(off, width)]
        kpos = off + lax.broadcasted_iota(jnp.int32, (1, width), 1)
        sel = (kk > thr) | ((kk == thr) & (kpos <= jrow))
        mask = sel & (kpos >= PAD_LEAD) & limit_fn(kpos)
        kc = k_ref[0, pl.ds(off, width), :]
        vc = v_ref[0, pl.ds(off, width), :]
        for h in range(N_HEADS):
            g = h // GROUP_HEADS
            s = lax.dot_general(q_ref[0, :, h * HEAD_DIM:(h + 1) * HEAD_DIM], kc[:, g * HEAD_DIM:(g + 1) * HEAD_DIM],
                                (((1,), (1,)), ((), ())), preferred_element_type=F32)
            s = jnp.where(mask, s + bias_fn(h), NEG)
            m_prev = m_sc[h]
            m_new = jnp.maximum(m_prev, jnp.max(s, axis=-1, keepdims=True))
            alpha = jnp.exp(m_prev - m_new)
            p = jnp.exp(s - m_new)
            l_sc[h] = alpha * l_sc[h] + jnp.sum(p, axis=-1, keepdims=True)
            pv = jnp.dot(p.astype(BF16), vc[:, g * HEAD_DIM:(g + 1) * HEAD_DIM], preferred_element_type=F32)
            sl = slice(h * HEAD_DIM, (h + 1) * HEAD_DIM)
            acc_sc[:, sl] = alpha * acc_sc[:, sl] + pv
            m_sc[h] = m_new

    near = jnp.maximum(i - 1, 0) * BLK

    @pl.when(i == 0)
    def _():
        attend(0, BLK, lambda kpos: kpos <= tpos, lambda h: tbl_ref[h, :, BLK:])

    @pl.when(i > 0)
    def _():
        attend(pl.multiple_of(near, LANES), 2 * BLK, lambda kpos: kpos <= tpos, lambda h: tbl_ref[h])

    def far_chunk(c, carry):
        attend(pl.multiple_of(c * ch, LANES), ch, lambda kpos: kpos < near, lambda h: relb_ref[far_bucket, h])
        return carry

    lax.fori_loop(0, (near + ch - 1) // ch, far_chunk, 0)

    for h in range(N_HEADS):
        sl = slice(h * HEAD_DIM, (h + 1) * HEAD_DIM)
        o_ref[0, :, sl] = acc_sc[:, sl] / l_sc[h]


def _dsa(c_bf, iw, rel_bias, tbl_c, top_k):
    b, p, _ = c_bf.shape
    nb = p // BLK
    ch = _tile(p, 640)
    far = _t5_bucket_np(np.arange(BLK + 1, p + 1))
    assert far.min() == far.max()
    kern = functools.partial(_dsa_kernel, ch=ch, top_k=top_k, far_bucket=int(far[0]))
    kv = LANES
    return pl.pallas_call(
        kern,
        out_shape=jax.ShapeDtypeStruct((b, p, D_GROUP), F32),
        grid=(b, nb),
        in_specs=[pl.BlockSpec(memory_space=pltpu.SMEM),
                  pl.BlockSpec((1, BLK, D_GROUP), lambda bb, i: (bb, i, 0)),
                  pl.BlockSpec((1, BLK, D_GROUP), lambda bb, i: (bb, i, 1)),
                  pl.BlockSpec((1, BLK, IDX_HEADS), lambda bb, i: (bb, i, 0)),
                  pl.BlockSpec((1, p, kv), lambda bb, i: (bb, 0, 2 * D_GROUP // kv)),
                  pl.BlockSpec((1, p, kv), lambda bb, i: (bb, 0, 2 * D_GROUP // kv + 1)),
                  pl.BlockSpec((1, p, kv), lambda bb, i: (bb, 0, 2 * D_GROUP // kv + 2)),
                  pl.BlockSpec((N_HEADS, BLK, 2 * BLK), lambda bb, i: (0, 0, 0))],
        out_specs=pl.BlockSpec((1, BLK, D_GROUP), lambda bb, i: (bb, i, 0)),
        scratch_shapes=[pltpu.VMEM((BLK, p), jnp.int32), pltpu.VMEM((BLK, 1), jnp.int32),
                        pltpu.VMEM((N_HEADS, BLK, 1), F32), pltpu.VMEM((N_HEADS, BLK, 1), F32),
                        pltpu.VMEM((BLK, D_GROUP), F32)],
        compiler_params=_params(("parallel", "arbitrary")),
        name="dsa",
    )(rel_bias, c_bf, c_bf, iw, c_bf, c_bf, c_bf, tbl_c)


def _swa_kernel(sink_ref, q_ref, kp_ref, kc_ref, vp_ref, vc_ref, tbl_ref, o_ref):
    i = pl.program_id(1)
    qi = lax.broadcasted_iota(jnp.int32, (BLK, 1), 0)
    kj = lax.broadcasted_iota(jnp.int32, (1, BLK), 1)
    mask_p = (kj > qi) & ((i - 1) * BLK + kj >= PAD_LEAD)
    mask_c = (kj <= qi) & (i * BLK + kj >= PAD_LEAD)
    nt = (((1,), (1,)), ((), ()))
    for h in range(N_HEADS):
        g = h // GROUP_HEADS
        gs = slice(g * HEAD_DIM, (g + 1) * HEAD_DIM)
        q = q_ref[0, :, h * HEAD_DIM:(h + 1) * HEAD_DIM]
        sp = lax.dot_general(q, kp_ref[0, :, gs], nt, preferred_element_type=F32) + tbl_ref[h, :, :BLK]
        scur = lax.dot_general(q, kc_ref[0, :, gs], nt, preferred_element_type=F32) + tbl_ref[h, :, BLK:]
        sp = jnp.where(mask_p, sp, NEG)
        scur = jnp.where(mask_c, scur, NEG)
        sink = sink_ref[h]
        m = jnp.maximum(jnp.maximum(jnp.max(sp, axis=-1, keepdims=True), jnp.max(scur, axis=-1, keepdims=True)), sink)
        ep = jnp.exp(sp - m)
        ec = jnp.exp(scur - m)
        den = jnp.sum(ep, axis=-1, keepdims=True) + jnp.sum(ec, axis=-1, keepdims=True) + jnp.exp(sink - m)
        out = (jnp.dot((ep / den).astype(BF16), vp_ref[0, :, gs], preferred_element_type=F32)
               + jnp.dot((ec / den).astype(BF16), vc_ref[0, :, gs], preferred_element_type=F32))
        o_ref[0, :, h * HEAD_DIM:(h + 1) * HEAD_DIM] = out


def _swa(d_bf, sinks_l, tbl_d):
    b, p, _ = d_bf.shape
    nb = p // BLK
    kcol = D_GROUP // LANES
    prev = lambda col: (lambda bb, i: (bb, jnp.maximum(i - 1, 0), col))
    cur = lambda col: (lambda bb, i: (bb, i, col))
    return pl.pallas_call(
        _swa_kernel,
        out_shape=jax.ShapeDtypeStruct((b, p, D_GROUP), F32),
        grid=(b, nb),
        in_specs=[pl.BlockSpec(memory_space=pltpu.SMEM),
                  pl.BlockSpec((1, BLK, D_GROUP), lambda bb, i: (bb, i, 0)),
                  pl.BlockSpec((1, BLK, LANES), prev(kcol)), pl.BlockSpec((1, BLK, LANES), cur(kcol)),
                  pl.BlockSpec((1, BLK, LANES), prev(kcol + 1)), pl.BlockSpec((1, BLK, LANES), cur(kcol + 1)),
                  pl.BlockSpec((N_HEADS, BLK, 2 * BLK), lambda bb, i: (0, 0, 0))],
        out_specs=pl.BlockSpec((1, BLK, D_GROUP), lambda bb, i: (bb, i, 0)),
        compiler_params=_params(("parallel", "parallel")),
        name="swa",
    )(sinks_l, d_bf, d_bf, d_bf, d_bf, d_bf, tbl_d)


def _outproj_kernel(oa_ref, ob_ref, oc_ref, od_ref, g_ref, x_ref, w_ref, lg_ref, lb_ref, o_ref):
    y = ALPHA * x_ref[...]
    for m, ref in enumerate((oa_ref, ob_ref, oc_ref, od_ref)):
        g = g_ref[:, m * D_GROUP:(m + 1) * D_GROUP]
        mixed = (ref[...] * (g * jax.nn.sigmoid(g))).astype(BF16)
        y = y + jnp.dot(mixed, w_ref[m * D_GROUP:(m + 1) * D_GROUP, :], preferred_element_type=F32)
    mu = jnp.mean(y, axis=-1, keepdims=True)
    yc = y - mu
    var = jnp.mean(yc * yc, axis=-1, keepdims=True)
    o_ref[...] = yc * lax.rsqrt(var + LN_EPS) * lg_ref[...] + lb_ref[...]


def _outproj(oa, ob, oc, od, gates, h2, w_out_bf, ln_g_l, ln_b_l):
    n, d = h2.shape
    tm = _tile(n, 640)
    row = lambda i: (i, 0)
    const = lambda i: (0, 0)
    return pl.pallas_call(
        _outproj_kernel,
        out_shape=jax.ShapeDtypeStruct((n, d), F32),
        grid=(n // tm,),
        in_specs=[pl.BlockSpec((tm, D_GROUP), row)] * 4
                 + [pl.BlockSpec((tm, W_G), row), pl.BlockSpec((tm, d), row),
                    pl.BlockSpec((W_G, d), const), pl.BlockSpec((1, d), const), pl.BlockSpec((1, d), const)],
        out_specs=pl.BlockSpec((tm, d), row),
        compiler_params=_params(("parallel",)),
        name="outproj_ln",
    )(oa, ob, oc, od, gates, h2, w_out_bf, ln_g_l.reshape(1, d), ln_b_l.reshape(1, d))


def kernel(x, meta_tokens, ln0_g, ln0_b, rel_bias, w_in, b_f, mla_gq, mla_gkv, w_uq, w_ukv, sinks, w_out, ln_g, ln_b):
    batch, seq, d = x.shape
    depth = w_in.shape[0]
    assert depth == DEPTH_FOR_NORM
    p = BLK + seq
    n = batch * p
    top_k = min(TOPK_MAX, seq // 4)

    lead = jnp.concatenate([jnp.zeros((batch, PAD_LEAD, d), x.dtype),
                            jnp.broadcast_to(meta_tokens[None].astype(x.dtype), (batch, N_META, d))], axis=1)
    h2 = jnp.concatenate([lead, x], axis=1).reshape(n, d)
    h2 = _layer_norm(h2, ln0_g, ln0_b)

    tbl = _bias_tables(rel_bias)
    tbl_c, tbl_d = tbl[:N_HEADS], tbl[N_HEADS:]
    c1, c2 = _rope_tables(p)

    for l in range(depth):
        w_p, wf_t = _prep_w_in(w_in[l])
        wq_p, wk_p, wv_p = _prep_mla_weights(w_uq[l], w_ukv[l])
        qkv_a, gates, mla_in, c_bf, iw, d_bf, ft = _inproj(h2, w_p, wf_t)

        ck = _decay(ft, b_f[l], batch, p)
        qkv_a3 = qkv_a.reshape(batch, p, W_A)
        out_a = _flash((qkv_a3, 0), (qkv_a3, 1), (qkv_a3, 2), ck, dk=HEAD_DIM, name="fox_attn")

        qb, kb, vb = _mla_prep(mla_in, mla_gq[l], mla_gkv[l], wq_p, wk_p, wv_p, c1, c2, p)
        hw = N_HEADS * LANES
        out_b = _flash((qb.reshape(batch, p, hw), 0), (kb.reshape(batch, p, hw), 0),
                       (vb.reshape(batch, p, D_GROUP), 0), None, dk=LANES, name="mla_attn")

        out_c = _dsa(c_bf.reshape(batch, p, W_C), iw.reshape(batch, p, IDX_HEADS), rel_bias, tbl_c, top_k)
        out_d = _swa(d_bf.reshape(batch, p, W_D), sinks[l], tbl_d)

        h2 = _outproj(out_a.reshape(n, D_GROUP), out_b.reshape(n, D_GROUP), out_c.reshape(n, D_GROUP),
                      out_d.reshape(n, D_GROUP), gates, h2, w_out[l].astype(BF16), ln_g[l], ln_b[l])

    return h2.reshape(batch, p, d)[:, BLK:]
```
